```python
import jax, jax.numpy as jnp
from jax import lax
import numpy as np

D_MODEL = 1024
BATCH = 8
SEQ = 2048
DEPTH = 1

RMS_EPS = 1e-6
HG_HEADS = 4
HG_DK = 128
HG_DV = 128
HG_WIDTH = 512
HG_CHUNK = 64
RW_HEADS = 8
RW_HEAD = 64
RW_WIDTH = 512
RW_DECAY_LORA = 64
RW_AAA_LORA = 64
RW_GATE_LORA = 128
RW_GN_EPS = 64e-5
RW_COLS = 1792
IN_WIDTHS = (512, 512, 512, 512, 1792, 1024, 1024)
IN_COLS = 5888
PK_HEADS = 8
PK_DIM = 128
PK_HALF = 64
N_KEYS = 128
N_EXPERTS = 16384
PK_TOPK = 16
PEER_BLOCK = 128
PLE_DIM = 256

kernel_name = "hgrn2_rwkv7_peer_hybrid_block"


def rms_norm(x, w):
    xf = x.astype(jnp.float32)
    y = xf * lax.rsqrt(jnp.mean(xf * xf, axis=-1, keepdims=True) + RMS_EPS)
    return (y * w.astype(jnp.float32)).astype(x.dtype)


def split_cols(cols, widths):
    out, start = [], 0
    for wd in widths:
        out.append(cols[..., start:start + wd])
        start += wd
    return out


def hgrn2_mixer(q, f_logit, i_val, g_out, lb, norm_w):
    B, S, _ = q.shape
    nc = S // HG_CHUNK
    f32 = jnp.float32
    q = jax.nn.silu(q.astype(f32)) * (HG_DK ** -0.5)
    f = lb + (1.0 - lb) * jax.nn.sigmoid(f_logit.astype(f32))
    log_f = jnp.log(f)
    k = 1.0 - f
    v = i_val.astype(f32)

    def to_chunks(t, d):
        return t.reshape(B, nc, HG_CHUNK, HG_HEADS, d).transpose(1, 0, 3, 2, 4)

    qc = to_chunks(q, HG_DK)
    kc = to_chunks(k, HG_DK)
    gc = to_chunks(log_f, HG_DK)
    vc = to_chunks(v, HG_DV)
    causal = jnp.tril(jnp.ones((HG_CHUNK, HG_CHUNK), dtype=bool))[:, :, None]

    def step(state, inp):
        q_c, k_c, v_c, g_c = inp
        b = jnp.cumsum(g_c, axis=2)
        o_inter = jnp.einsum('bhtk,bhkv->bhtv', q_c * jnp.exp(b), state)
        diff = b[:, :, :, None, :] - b[:, :, None, :, :]
        dec = jnp.where(causal, jnp.exp(jnp.where(causal, diff, 0.0)), 0.0)
        att = jnp.einsum('bhtk,bhsk,bhtsk->bhts', q_c, k_c, dec)
        o_intra = jnp.einsum('bhts,bhsv->bhtv', att, v_c)
        b_last = b[:, :, -1:, :]
        new_state = jnp.exp(b_last[:, :, 0, :])[..., None] * state + jnp.einsum('bhsk,bhsv->bhkv', k_c * jnp.exp(b_last - b), v_c)
        return new_state, o_inter + o_intra

    s0 = jnp.zeros((B, HG_HEADS, HG_DK, HG_DV), f32)
    _, o = lax.scan(step, s0, (qc, kc, vc, gc))
    o = o.transpose(1, 0, 3, 2, 4).reshape(B, S, HG_HEADS, HG_DV)
    o = o * lax.rsqrt(jnp.mean(o * o, axis=-1, keepdims=True) + RMS_EPS)
    o = o.reshape(B, S, HG_WIDTH) * norm_w.astype(f32) * jax.nn.silu(g_out.astype(f32))
    return o.astype(g_out.dtype)


def rwkv7_mixer(rw, mu, w0, w2, a0, a2, g2, k_k, k_a, r_k, ln_w, ln_b):
    B, S, _ = rw.shape
    f32 = jnp.float32
    prev = jnp.pad(rw, ((0, 0), (1, 0), (0, 0)))[:, :S]
    rw = rw + (prev - rw) * mu
    r, k, v, wd, ad, gd = split_cols(rw, (RW_WIDTH, RW_WIDTH, RW_WIDTH, RW_DECAY_LORA, RW_AAA_LORA, RW_GATE_LORA))
    w_log = -jax.nn.softplus(-(w0 + jnp.tanh(wd) @ w2).astype(f32)) - 0.5
    decay = jnp.exp(-jnp.exp(w_log))
    a = jax.nn.sigmoid((a0 + ad @ a2).astype(f32))
    g = (jax.nn.sigmoid(gd) @ g2).astype(f32)

    def heads(t):
        return t.astype(f32).reshape(B, S, RW_HEADS, RW_HEAD)

    r = heads(r)
    k = heads(k)
    v = heads(v)
    decay = heads(decay)
    a = heads(a)
    kk = k * k_k.astype(f32).reshape(RW_HEADS, RW_HEAD)
    kk = kk / jnp.maximum(jnp.sqrt(jnp.sum(kk * kk, axis=-1, keepdims=True)), 1e-12)
    k = k * (1.0 + (a - 1.0) * k_a.astype(f32).reshape(RW_HEADS, RW_HEAD))

    def step(state, inp):
        r_t, w_t, k_t, v_t, kk_t, a_t = inp
        sa = jnp.einsum('bhvk,bhk->bhv', state, kk_t)
        state = state * w_t[:, :, None, :] - sa[..., None] * (kk_t * a_t)[:, :, None, :] + v_t[..., None] * k_t[:, :, None, :]
        return state, jnp.einsum('bhvk,bhk->bhv', state, r_t)

    xs = (jnp.moveaxis(r, 1, 0), jnp.moveaxis(decay, 1, 0), jnp.moveaxis(k, 1, 0), jnp.moveaxis(v, 1, 0), jnp.moveaxis(kk, 1, 0), jnp.moveaxis(a, 1, 0))
    s0 = jnp.zeros((B, RW_HEADS, RW_HEAD, RW_HEAD), f32)
    _, y = lax.scan(step, s0, xs)
    y = jnp.moveaxis(y, 0, 1)
    mean = jnp.mean(y, axis=-1, keepdims=True)
    var = jnp.mean(jnp.square(y - mean), axis=-1, keepdims=True)
    y = ((y - mean) * lax.rsqrt(var + RW_GN_EPS)).reshape(B, S, RW_WIDTH) * ln_w.astype(f32) + ln_b.astype(f32)
    bonus = jnp.sum(r * k * r_k.astype(f32).reshape(RW_HEADS, RW_HEAD), axis=-1, keepdims=True) * v
    y = (y + bonus.reshape(B, S, RW_WIDTH)) * g
    return y.astype(rw.dtype)


def peer_ffn(h, wq, sub_keys, u_emb, v_emb):
    B, S, D = h.shape
    f32 = jnp.float32
    h_blocks = h.reshape(B * S // PEER_BLOCK, PEER_BLOCK, D)

    def block(hb):
        tb = hb.shape[0]
        q = (hb @ wq).reshape(tb, PK_HEADS, 2, PK_HALF)
        sc = jnp.einsum('thcd,cnd->thcn', q, sub_keys).astype(f32)
        s_top, i_top = lax.top_k(sc, PK_TOPK)
        cand = s_top[:, :, 0, :, None] + s_top[:, :, 1, None, :]
        c_s, c_i = lax.top_k(cand.reshape(tb, PK_HEADS, PK_TOPK * PK_TOPK), PK_TOPK)
        idx1 = jnp.take_along_axis(i_top[:, :, 0], c_i // PK_TOPK, axis=-1)
        idx2 = jnp.take_along_axis(i_top[:, :, 1], c_i % PK_TOPK, axis=-1)
        idx = idx1 * N_KEYS + idx2
        gate = jax.nn.softmax(c_s, axis=-1)
        u = jnp.take(u_emb, idx, axis=0)
        act = jax.nn.gelu(jnp.einsum('td,thkd->thk', hb, u).astype(f32), approximate=False)
        v_sel = jnp.take(v_emb, idx, axis=0)
        return jnp.einsum('thk,thkd->td', (gate * act).astype(hb.dtype), v_sel)

    return lax.map(block, h_blocks).reshape(B, S, D)


def setup_inputs(seed: int = 0) -> dict:
    key = jax.random.key(seed)
    ks = jax.random.split(key, 32)
    f32 = jnp.float32

    def nrm(k, shape, scale):
        return jax.random.normal(k, shape, f32) * scale

    def gain(k, shape):
        return 1.0 + 0.02 * jax.random.normal(k, shape, f32)

    return {
        "x": nrm(ks[0], (BATCH, SEQ, D_MODEL), 1.0),
        "p": nrm(ks[1], (DEPTH, BATCH, SEQ, PLE_DIM), 1.0),
        "norm_mix_w": gain(ks[2], (DEPTH, D_MODEL)),
        "w_in": nrm(ks[3], (DEPTH, D_MODEL, IN_COLS), D_MODEL ** -0.5),
        "hg_lb": nrm(ks[4], (DEPTH + 1, HG_WIDTH), 0.1),
        "hg_norm_w": gain(ks[5], (DEPTH, HG_WIDTH)),
        "rw_mu": jax.random.uniform(ks[6], (DEPTH, RW_COLS), f32),
        "rw_w0": jax.random.uniform(ks[7], (DEPTH, RW_WIDTH), f32, minval=-5.5, maxval=-0.5),
        "rw_w2": nrm(ks[8], (DEPTH, RW_DECAY_LORA, RW_WIDTH), 0.1),
        "rw_a0": nrm(ks[9], (DEPTH, RW_WIDTH), 0.1),
        "rw_a2": nrm(ks[10], (DEPTH, RW_AAA_LORA, RW_WIDTH), 0.1),
        "rw_g2": nrm(ks[11], (DEPTH, RW_GATE_LORA, RW_WIDTH), RW_GATE_LORA ** -0.5),
        "rw_k_k": 0.85 + nrm(ks[12], (DEPTH, RW_WIDTH), 0.02),
        "rw_k_a": gain(ks[13], (DEPTH, RW_WIDTH)),
        "rw_r_k": nrm(ks[14], (DEPTH, RW_WIDTH), 0.1),
        "rw_ln_w": gain(ks[15], (DEPTH, RW_WIDTH)),
        "rw_ln_b": nrm(ks[16], (DEPTH, RW_WIDTH), 0.02),
        "w_branch_hg": nrm(ks[17], (DEPTH, HG_WIDTH, D_MODEL), HG_WIDTH ** -0.5),
        "w_branch_rw": nrm(ks[18], (DEPTH, RW_WIDTH, D_MODEL), RW_WIDTH ** -0.5),
        "w_out": nrm(ks[19], (DEPTH, D_MODEL, D_MODEL), D_MODEL ** -0.5),
        "norm_ffn_w": gain(ks[20], (DEPTH, D_MODEL)),
        "peer_wq": nrm(ks[21], (DEPTH, D_MODEL, PK_HEADS * PK_DIM), D_MODEL ** -0.5),
        "peer_keys": nrm(ks[22], (DEPTH, 2, N_KEYS, PK_HALF), PK_HALF ** -0.5),
        "peer_u": nrm(ks[23], (DEPTH, N_EXPERTS, D_MODEL), D_MODEL ** -0.5),
        "peer_v": nrm(ks[24], (DEPTH, N_EXPERTS, D_MODEL), PK_HEADS ** -0.5),
        "norm_ple_w": gain(ks[25], (DEPTH, D_MODEL)),
        "ple_proj": nrm(ks[26], (DEPTH, PLE_DIM, D_MODEL), PLE_DIM ** -0.5),
        "ple_gate": nrm(ks[27], (DEPTH, D_MODEL, D_MODEL), D_MODEL ** -0.5),
        "final_norm_w": gain(ks[28], (D_MODEL,)),
    }


def reference(x, p, norm_mix_w, w_in, hg_lb, hg_norm_w, rw_mu, rw_w0, rw_w2, rw_a0, rw_a2, rw_g2, rw_k_k, rw_k_a, rw_r_k, rw_ln_w, rw_ln_b, w_branch_hg, w_branch_rw, w_out, norm_ffn_w, peer_wq, peer_keys, peer_u, peer_v, norm_ple_w, ple_proj, ple_gate, final_norm_w):
    lower_bounds = jnp.cumsum(jax.nn.softmax(hg_lb.astype(jnp.float32), axis=0), axis=0)
    for i in range(DEPTH):
        h = rms_norm(x, norm_mix_w[i])
        cols = h @ w_in[i]
        hq, hf, hi, hgo, rw_cols, gate_hg, gate_rw = split_cols(cols, IN_WIDTHS)
        y_hg = hgrn2_mixer(hq, hf, hi, hgo, lower_bounds[i], hg_norm_w[i])
        y_rw = rwkv7_mixer(rw_cols, rw_mu[i], rw_w0[i], rw_w2[i], rw_a0[i], rw_a2[i], rw_g2[i], rw_k_k[i], rw_k_a[i], rw_r_k[i], rw_ln_w[i], rw_ln_b[i])
        merged = jax.nn.sigmoid(gate_hg) * (y_hg @ w_branch_hg[i]) + jax.nn.sigmoid(gate_rw) * (y_rw @ w_branch_rw[i])
        x = x + merged @ w_out[i]
        x = x + peer_ffn(rms_norm(x, norm_ffn_w[i]), peer_wq[i], peer_keys[i], peer_u[i], peer_v[i])
        ple_g = jax.nn.sigmoid(rms_norm(x, norm_ple_w[i]) @ ple_gate[i])
        x = x + ple_g * (p[i] @ ple_proj[i])
    return rms_norm(x, final_norm_w)
```

```python
import functools

import jax
import jax.numpy as jnp
from jax import lax
from jax.experimental import pallas as pl
from jax.experimental.pallas import tpu as pltpu

F32 = jnp.float32
BF16 = jnp.bfloat16
HIGHEST = lax.Precision.HIGHEST

RMS_EPS = 1e-6
HG_HEADS, HG_DK, HG_CHUNK, HG_WIDTH = 4, 128, 64, 512
RW_HEADS, RW_HEAD, RW_WIDTH, RW_CHUNK, RW_COLS = 8, 64, 512, 64, 1792
RW_GN_EPS = 64e-5
RW_SUB = 16
PK_HEADS, PK_HALF, N_KEYS, PK_TOPK = 8, 64, 128, 16
HG_COLS, GATE_COLS = 2048, 2048

NT = (((1,), (1,)), ((), ()))
TN = (((0,), (0,)), ((), ()))


def _dot(a, b, dims=None, precision=None):
    if dims is None:
        return jnp.dot(a, b, preferred_element_type=F32, precision=precision)
    return lax.dot_general(a, b, dims, preferred_element_type=F32, precision=precision)


def _bdot(a, b, dims=None):
    return _dot(a.astype(BF16), b.astype(BF16), dims)


def _rms(x, w):
    return x * lax.rsqrt(jnp.mean(x * x, axis=-1, keepdims=True) + RMS_EPS) * w


def _sigmoid(x):
    return 1.0 / (1.0 + jnp.exp(-x))


def _tri(n, strict):
    r = lax.broadcasted_iota(jnp.int32, (n, n), 0)
    c = lax.broadcasted_iota(jnp.int32, (n, n), 1)
    return (c < r) if strict else (c <= r)


def _cumsum_rows(x):
    n = x.shape[0]
    return _dot(_tri(n, False).astype(F32), x, precision=HIGHEST)


def _inproj_kernel(x_ref, nw_ref, w_ref, hg_ref, rw_ref, gt_ref):
    h = _rms(x_ref[...], nw_ref[...]).astype(BF16)

    def emit(out_ref, col0, width):
        for c in range(0, width, 256):
            out_ref[:, c:c + 256] = _dot(h, w_ref[:, col0 + c:col0 + c + 256])

    emit(hg_ref, 0, HG_COLS)
    emit(rw_ref, HG_COLS, RW_COLS)
    emit(gt_ref, HG_COLS + RW_COLS, GATE_COLS)


def _inproj(x2, nw, w_bf, tm=256):
    T, D = x2.shape
    const = lambda i: (0, 0)
    return pl.pallas_call(
        _inproj_kernel,
        grid=(T // tm,),
        in_specs=[pl.BlockSpec((tm, D), lambda i: (i, 0)),
                  pl.BlockSpec((1, D), const),
                  pl.BlockSpec(w_bf.shape, const)],
        out_specs=[pl.BlockSpec((tm, HG_COLS), lambda i: (i, 0)),
                   pl.BlockSpec((tm, RW_COLS), lambda i: (i, 0)),
                   pl.BlockSpec((tm, GATE_COLS), lambda i: (i, 0))],
        out_shape=[jax.ShapeDtypeStruct((T, HG_COLS), F32),
                   jax.ShapeDtypeStruct((T, RW_COLS), F32),
                   jax.ShapeDtypeStruct((T, GATE_COLS), F32)],
        compiler_params=pltpu.CompilerParams(dimension_semantics=("arbitrary",)),
        name="inproj",
    )(x2, nw, w_bf)


def _hgrn_kernel(q_ref, f_ref, i_ref, go_ref, lb_ref, nw_ref, o_ref, st_ref):
    @pl.when(pl.program_id(1) == 0)
    def _():
        st_ref[...] = jnp.zeros_like(st_ref)

    C = HG_CHUNK
    lb = lb_ref[...]
    q, fl, v, go = q_ref[0], f_ref[0], i_ref[0], go_ref[0]
    qs = q * _sigmoid(q) * (HG_DK ** -0.5)
    f = lb + (1.0 - lb) * _sigmoid(fl)
    k = 1.0 - f
    b = _cumsum_rows(jnp.log(f))
    b_last = b[C - 1:C, :]
    qe = qs * jnp.exp(b)
    ke = k * jnp.exp(-b)
    kd = k * jnp.exp(b_last - b)
    dec = jnp.exp(b_last)
    causal = _tri(C, False)
    outs = []
    for h in range(HG_HEADS):
        sl = slice(h * HG_DK, (h + 1) * HG_DK)
        st = st_ref[h]
        qh = qe[:, sl].astype(BF16)
        vh = v[:, sl].astype(BF16)
        att = jnp.where(causal, _bdot(qh, ke[:, sl], NT), 0.0)
        o = _bdot(qh, st, NT) + _bdot(att, vh)
        st_ref[h] = st * dec[:, sl] + _bdot(vh, kd[:, sl], TN)
        outs.append(o * lax.rsqrt(jnp.mean(o * o, axis=-1, keepdims=True) + RMS_EPS))
    o = jnp.concatenate(outs, axis=1) * nw_ref[...] * (go * _sigmoid(go))
    o_ref[0] = o.astype(o_ref.dtype)


def _hgrn(hg3, lb, nw):
    B, S, _ = hg3.shape
    C = HG_CHUNK
    spec = lambda j: pl.BlockSpec((1, C, HG_WIDTH), lambda b, c, j=j: (b, c, j))
    const = lambda b, c: (0, 0)
    return pl.pallas_call(
        _hgrn_kernel,
        grid=(B, S // C),
        in_specs=[spec(0), spec(1), spec(2), spec(3),
                  pl.BlockSpec((1, HG_WIDTH), const), pl.BlockSpec((1, HG_WIDTH), const)],
        out_specs=pl.BlockSpec((1, C, HG_WIDTH), lambda b, c: (b, c, 0)),
        out_shape=jax.ShapeDtypeStruct((B, S, HG_WIDTH), BF16),
        scratch_shapes=[pltpu.VMEM((HG_HEADS, HG_DK, HG_DK), F32)],
        compiler_params=pltpu.CompilerParams(dimension_semantics=("arbitrary", "arbitrary")),
        name="hgrn2",
    )(hg3, hg3, hg3, hg3, lb, nw)


def _softplus(z):
    return jnp.maximum(z, 0.0) + jnp.log(1.0 + jnp.exp(-jnp.abs(z)))


def _head_sum(x):
    n = x.shape[1]
    r = lax.broadcasted_iota(jnp.int32, (n, n), 0) // RW_HEAD
    c = lax.broadcasted_iota(jnp.int32, (n, n), 1) // RW_HEAD
    return _dot(x, (r == c).astype(F32), precision=HIGHEST)


def _unit_lower_solve(a, x):
    C = a.shape[0]
    r = lax.broadcasted_iota(jnp.int32, (C, C), 0)
    c = lax.broadcasted_iota(jnp.int32, (C, C), 1)
    same = (r // RW_SUB) == (c // RW_SUB)
    d = jnp.where(same, a, 0.0)
    n = jnp.where(same, 0.0, a)
    eye = (r == c).astype(F32)
    t = eye + d
    p = d
    steps = 1
    while 2 * steps < RW_SUB:
        p = _bdot(p, p)
        t = t + _bdot(t, p)
        steps *= 2
    e = _bdot(t, n)
    y = _bdot(t, x)
    blocks = C // RW_SUB
    steps = 1
    while steps < blocks:
        y = y + _bdot(e, y)
        steps *= 2
        if steps < blocks:
            e = _bdot(e, e)
    return y


def _rwkv_kernel(rw_ref, prev_ref, mu_ref, w0_ref, w2_ref, a0_ref, a2_ref, g2_ref, kk_ref, ka_ref, rk_ref,
                 lnw_ref, lnb_ref, o_ref, st_ref):
    ci = pl.program_id(1)

    @pl.when(ci == 0)
    def _():
        st_ref[...] = jnp.zeros_like(st_ref)

    C, N, W = RW_CHUNK, RW_HEAD, RW_WIDTH
    x = rw_ref[0]
    last_prev = jnp.where(ci == 0, 0.0, prev_ref[0][7:8, :])
    rows = lax.broadcasted_iota(jnp.int32, x.shape, 0)
    prev = jnp.where(rows == 0, last_prev, pltpu.roll(x, 1, axis=0))
    xs = x + (prev - x) * mu_ref[...]
    r, k, v = xs[:, 0:W], xs[:, W:2 * W], xs[:, 2 * W:3 * W]
    wd, ad, gd = xs[:, 3 * W:3 * W + 64], xs[:, 3 * W + 64:3 * W + 128], xs[:, 3 * W + 128:3 * W + 256]
    w_log = -_softplus(-(w0_ref[...] + _bdot(jnp.tanh(wd), w2_ref[...]))) - 0.5
    lw = -jnp.exp(w_log)
    a = _sigmoid(a0_ref[...] + _bdot(ad, a2_ref[...]))
    g = _bdot(_sigmoid(gd), g2_ref[...])
    kk = k * kk_ref[...]
    kk = kk / jnp.maximum(jnp.sqrt(_head_sum(kk * kk)), 1e-12)
    k = k * (1.0 + (a - 1.0) * ka_ref[...])
    bonus = _head_sum(r * k * rk_ref[...]) * v

    al = -kk
    be = kk * a
    cum = _cumsum_rows(lw)
    cum_last = cum[C - 1:C, :]
    e_pos = jnp.exp(cum)
    e_neg = jnp.exp(-cum)
    e_end = jnp.exp(cum_last - cum)
    al_t = al * jnp.exp(cum - lw)
    r_t = r * e_pos
    be_t = be * e_neg
    k_t = k * e_neg
    be_e = be * e_end
    k_e = k * e_end
    gam = jnp.exp(cum_last)

    strict = _tri(C, True)
    incl = _tri(C, False)
    eye = lax.broadcasted_iota(jnp.int32, (N, N), 0) == lax.broadcasted_iota(jnp.int32, (N, N), 1)
    outs = []
    for h in range(RW_HEADS):
        sl = slice(h * N, (h + 1) * N)
        vh = v[:, sl]
        lhs = jnp.concatenate([al_t[:, sl], r_t[:, sl]], axis=0)
        rhs = jnp.concatenate([be_t[:, sl], k_t[:, sl]], axis=0)
        p = _bdot(lhs, rhs, NT)
        a_m = jnp.where(strict, p[:C, :C], 0.0)
        b_m = jnp.where(strict, p[:C, C:], 0.0)
        aq = jnp.where(incl, p[C:, :C], 0.0)
        bq = jnp.where(incl, p[C:, C:], 0.0)
        z = _unit_lower_solve(a_m, jnp.concatenate([al_t[:, sl], _bdot(b_m, vh)], axis=1))
        aqz = _bdot(aq, z)
        rq = r_t[:, sl] + aqz[:, :N]
        y_loc = aqz[:, N:] + _bdot(bq, vh)
        btz = _bdot(be_e[:, sl], z, TN)
        phi = jnp.where(eye, gam[:, sl], 0.0) + btz[:, :N]
        psi = btz[:, N:] + _bdot(k_e[:, sl], vh, TN)
        st = st_ref[h]
        y = _dot(rq, st, precision=HIGHEST) + y_loc
        st_ref[h] = _dot(phi, st, precision=HIGHEST) + psi
        mean = jnp.mean(y, axis=-1, keepdims=True)
        yc = y - mean
        var = jnp.mean(yc * yc, axis=-1, keepdims=True)
        outs.append(yc * lax.rsqrt(var + RW_GN_EPS))
    y = jnp.concatenate(outs, axis=1) * lnw_ref[...] + lnb_ref[...]
    o_ref[0] = ((y + bonus) * g).astype(o_ref.dtype)


def _rwkv(rw3, mu, w0, w2, a0, a2, g2, k_k, k_a, r_k, ln_w, ln_b):
    B, S, _ = rw3.shape
    C = RW_CHUNK
    const = lambda b, c: (0, 0)
    full = lambda arr: pl.BlockSpec(arr.shape, const)
    return pl.pallas_call(
        _rwkv_kernel,
        grid=(B, S // C),
        in_specs=[pl.BlockSpec((1, C, RW_COLS), lambda b, c: (b, c, 0)),
                  pl.BlockSpec((1, 8, RW_COLS), lambda b, c: (b, jnp.maximum(c * (C // 8) - 1, 0), 0)),
                  full(mu), full(w0), full(w2), full(a0), full(a2), full(g2), full(k_k), full(k_a), full(r_k),
                  full(ln_w), full(ln_b)],
        out_specs=pl.BlockSpec((1, C, RW_WIDTH), lambda b, c: (b, c, 0)),
        out_shape=jax.ShapeDtypeStruct((B, S, RW_WIDTH), BF16),
        scratch_shapes=[pltpu.VMEM((RW_HEADS, RW_HEAD, RW_HEAD), F32)],
        compiler_params=pltpu.CompilerParams(dimension_semantics=("arbitrary", "arbitrary")),
        name="rwkv7",
    )(rw3, rw3, mu, w0, w2, a0, a2, g2, k_k, k_a, r_k, ln_w, ln_b)


def _merge_kernel(x_ref, yhg_ref, yrw_ref, gt_ref, wbh_ref, wbr_ref, wout_ref, nw_ref, wqt_ref,
                  x1_ref, h2_ref, qt_ref):
    D = x_ref.shape[1]
    merged = (_sigmoid(gt_ref[:, :D]) * _dot(yhg_ref[...], wbh_ref[...])
              + _sigmoid(gt_ref[:, D:]) * _dot(yrw_ref[...], wbr_ref[...]))
    x1 = x_ref[...] + _dot(merged.astype(BF16), wout_ref[...])
    x1_ref[...] = x1
    h2 = _rms(x1, nw_ref[...]).astype(BF16)
    h2_ref[...] = h2
    qt_ref[...] = _dot(wqt_ref[...], h2, NT)


def _merge(x2, yhg, yrw, gt, wbh, wbr, wout, nw, wqt, tm=256):
    T, D = x2.shape
    const = lambda i: (0, 0)
    row = lambda w: pl.BlockSpec((tm, w), lambda i: (i, 0))
    full = lambda arr: pl.BlockSpec(arr.shape, const)
    return pl.pallas_call(
        _merge_kernel,
        grid=(T // tm,),
        in_specs=[row(D), row(HG_WIDTH), row(RW_WIDTH), row(GATE_COLS),
                  full(wbh), full(wbr), full(wout), full(nw), full(wqt)],
        out_specs=[row(D), row(D), pl.BlockSpec((wqt.shape[0], tm), lambda i: (0, i))],
        out_shape=[jax.ShapeDtypeStruct((T, D), F32), jax.ShapeDtypeStruct((T, D), BF16),
                   jax.ShapeDtypeStruct((wqt.shape[0], T), F32)],
        compiler_params=pltpu.CompilerParams(dimension_semantics=("arbitrary",)),
        name="merge",
    )(x2, yhg, yrw, gt, wbh, wbr, wout, nw, wqt)


def _top16_rows(s, payload=None):
    R = s.shape[0]
    rows = lax.broadcasted_iota(jnp.int32, s.shape, 0)
    vals, idxs = [], []
    for _ in range(PK_TOPK):
        m = jnp.max(s, axis=0, keepdims=True)
        idx = jnp.min(jnp.where(s == m, rows, R), axis=0, keepdims=True)
        sel = rows == idx
        vals.append(m)
        idxs.append(idx if payload is None else jnp.max(jnp.where(sel, payload, -1), axis=0, keepdims=True))
        s = jnp.where(sel, -jnp.inf, s)
    return jnp.concatenate(vals, axis=0), jnp.concatenate(idxs, axis=0)


def _select_kernel(qt_ref, keys_ref, i1_ref, i2_ref, g_ref):
    qt = qt_ref[...]
    s1 = _dot(keys_ref[0], qt[:PK_HALF, :], precision=HIGHEST)
    s2 = _dot(keys_ref[1], qt[PK_HALF:, :], precision=HIGHEST)
    a, ia = _top16_rows(s1)
    b, ib = _top16_rows(s2)
    cand = jnp.concatenate([a[i:i + 1, :] + b for i in range(PK_TOPK)], axis=0)
    eid = jnp.concatenate([ia[i:i + 1, :] * N_KEYS + ib for i in range(PK_TOPK)], axis=0)
    cs, ce = _top16_rows(cand, eid)
    ex = jnp.exp(cs - cs[0:1, :])
    g_ref[...] = ex / jnp.sum(ex, axis=0, keepdims=True)
    i1_ref[...] = (ce // N_KEYS).astype(F32)
    i2_ref[...] = (ce % N_KEYS).astype(F32)


def _select(qt, keys, tl=256):
    R, T = qt.shape
    heads = R // (2 * PK_HALF)
    out = jax.ShapeDtypeStruct((heads * PK_TOPK, T), F32)
    ospec = pl.BlockSpec((PK_TOPK, tl), lambda i, h: (h, i))
    return pl.pallas_call(
        _select_kernel,
        grid=(T // tl, heads),
        in_specs=[pl.BlockSpec((2 * PK_HALF, tl), lambda i, h: (h, i)),
                  pl.BlockSpec(keys.shape, lambda i, h: (0, 0, 0))],
        out_specs=[ospec, ospec, ospec],
        out_shape=[out, out, out],
        compiler_params=pltpu.CompilerParams(dimension_semantics=("arbitrary", "arbitrary")),
        name="peer_select",
    )(qt, keys)


def _gbuild_kernel(i1_ref, i2_ref, g_ref, o_ref, i1s, i2s, gs):
    i1s[...] = i1_ref[...].T
    i2s[...] = i2_ref[...].T
    gs[...] = g_ref[...].T
    n = N_KEYS
    sub = lax.broadcasted_iota(jnp.int32, (n, i1s.shape[1]), 0).astype(F32)

    def body(t, carry):
        r1 = i1s[pl.ds(t, 1), :]
        r2 = i2s[pl.ds(t, 1), :]
        gg = gs[pl.ds(t, 1), :]
        at = jnp.where(sub == r1, gg, 0.0).astype(BF16)
        bt = jnp.where(sub == r2, 1.0, 0.0).astype(BF16)
        o_ref[t] = _dot(at, bt, NT)
        return carry

    lax.fori_loop(0, o_ref.shape[0], body, 0, unroll=4)


def _gbuild(i1, i2, g, tg=128):
    J, T = i1.shape
    ispec = pl.BlockSpec((J, tg), lambda i: (0, i))
    return pl.pallas_call(
        _gbuild_kernel,
        grid=(T // tg,),
        in_specs=[ispec, ispec, ispec],
        out_specs=pl.BlockSpec((tg, N_KEYS, N_KEYS), lambda i: (i, 0, 0)),
        out_shape=jax.ShapeDtypeStruct((T, N_KEYS, N_KEYS), F32),
        scratch_shapes=[pltpu.VMEM((tg, J), F32)] * 3,
        compiler_params=pltpu.CompilerParams(dimension_semantics=("arbitrary",)),
        name="peer_gates",
    )(i1, i2, g)


def _gelu(x):
    return 0.5 * x * (1.0 + lax.erf(x * (2.0 ** -0.5)))


def _peer_kernel(h_ref, g_ref, u_ref, v_ref, x1_ref, o_ref, acc_ref, w_ref):
    j = pl.program_id(1)

    @pl.when(j == 0)
    def _():
        acc_ref[...] = jnp.zeros_like(acc_ref)

    act = _dot(h_ref[...], u_ref[...], NT)
    for c in range(g_ref.shape[1]):
        sl = slice(c * N_KEYS, (c + 1) * N_KEYS)
        w_ref[:, sl] = (g_ref[:, c, :] * _gelu(act[:, sl])).astype(BF16)
    acc_ref[...] += _dot(w_ref[...], v_ref[...])

    @pl.when(j == pl.num_programs(1) - 1)
    def _():
        o_ref[...] = x1_ref[...] + acc_ref[...]


def _peer(h2, g3, u_bf, v_bf, x1, tm=512, te=1024):
    T, D = h2.shape
    E = u_bf.shape[0]
    return pl.pallas_call(
        _peer_kernel,
        grid=(T // tm, E // te),
        in_specs=[pl.BlockSpec((tm, D), lambda i, j: (i, 0)),
                  pl.BlockSpec((tm, te // N_KEYS, N_KEYS), lambda i, j: (i, j, 0)),
                  pl.BlockSpec((te, D), lambda i, j: (j, 0)),
                  pl.BlockSpec((te, D), lambda i, j: (j, 0)),
                  pl.BlockSpec((tm, D), lambda i, j: (i, 0))],
        out_specs=pl.BlockSpec((tm, D), lambda i, j: (i, 0)),
        out_shape=jax.ShapeDtypeStruct((T, D), F32),
        scratch_shapes=[pltpu.VMEM((tm, D), F32), pltpu.VMEM((tm, te), BF16)],
        compiler_params=pltpu.CompilerParams(dimension_semantics=("arbitrary", "arbitrary")),
        name="peer_dense",
    )(h2, g3, u_bf, v_bf, x1)


def _ple_kernel(x_ref, p_ref, nw_ref, wg_ref, wp_ref, fw_ref, o_ref, *, final):
    x = x_ref[...]
    gate = _sigmoid(_dot(_rms(x, nw_ref[...]).astype(BF16), wg_ref[...]))
    x = x + gate * _dot(p_ref[...].astype(BF16), wp_ref[...])
    o_ref[...] = _rms(x, fw_ref[...]) if final else x


def _ple(x2, p2, nw, wg, wp, fw, final, tm=512):
    T, D = x2.shape
    const = lambda i: (0, 0)
    full = lambda arr: pl.BlockSpec(arr.shape, const)
    return pl.pallas_call(
        functools.partial(_ple_kernel, final=final),
        grid=(T // tm,),
        in_specs=[pl.BlockSpec((tm, D), lambda i: (i, 0)), pl.BlockSpec((tm, p2.shape[1]), lambda i: (i, 0)),
                  full(nw), full(wg), full(wp), full(fw)],
        out_specs=pl.BlockSpec((tm, D), lambda i: (i, 0)),
        out_shape=jax.ShapeDtypeStruct((T, D), F32),
        compiler_params=pltpu.CompilerParams(dimension_semantics=("arbitrary",)),
        name="ple_final",
    )(x2, p2, nw, wg, wp, fw)


def kernel(x, p, norm_mix_w, w_in, hg_lb, hg_norm_w, rw_mu, rw_w0, rw_w2, rw_a0, rw_a2, rw_g2, rw_k_k, rw_k_a, rw_r_k, rw_ln_w, rw_ln_b, w_branch_hg, w_branch_rw, w_out, norm_ffn_w, peer_wq, peer_keys, peer_u, peer_v, norm_ple_w, ple_proj, ple_gate, final_norm_w):
    B, S, D = x.shape
    T = B * S
    depth = w_in.shape[0]
    row = lambda a: a.reshape(1, -1).astype(F32)
    lower_bounds = jnp.cumsum(jax.nn.softmax(hg_lb.astype(F32), axis=0), axis=0)
    xf = x.reshape(T, D)
    for i in range(depth):
        hg, rw, gt = _inproj(xf, row(norm_mix_w[i]), w_in[i].astype(BF16))
        y_hg = _hgrn(hg.reshape(B, S, HG_COLS), row(lower_bounds[i]), row(hg_norm_w[i]))
        y_rw = _rwkv(rw.reshape(B, S, RW_COLS), row(rw_mu[i]), row(rw_w0[i]), rw_w2[i].astype(BF16),
                     row(rw_a0[i]), rw_a2[i].astype(BF16), rw_g2[i].astype(BF16), row(rw_k_k[i]),
                     row(rw_k_a[i]), row(rw_r_k[i]), row(rw_ln_w[i]), row(rw_ln_b[i]))
        x1, h2, qt = _merge(xf, y_hg.reshape(T, HG_WIDTH), y_rw.reshape(T, RW_WIDTH), gt,
                            w_branch_hg[i].astype(BF16), w_branch_rw[i].astype(BF16), w_out[i].astype(BF16),
                            row(norm_ffn_w[i]), peer_wq[i].T.astype(BF16))
        i1, i2, g = _select(qt, peer_keys[i].astype(F32))
        g3 = _gbuild(i1, i2, g)
        x2 = _peer(h2, g3, peer_u[i].astype(BF16), peer_v[i].astype(BF16), x1)
        xf = _ple(x2, p[i].reshape(T, -1), row(norm_ple_w[i]), ple_gate[i].astype(BF16),
                  ple_proj[i].astype(BF16), row(final_norm_w), final=(i == depth - 1))
    return xf.reshape(B, S, D)
```

```python
import functools

import jax
import jax.numpy as jnp
from jax import lax
from jax.experimental import pallas as pl
from jax.experimental.pallas import tpu as pltpu

F32 = jnp.float32
BF16 = jnp.bfloat16
HIGHEST = lax.Precision.HIGHEST

RMS_EPS = 1e-6
HG_HEADS, HG_DK, HG_CHUNK, HG_WIDTH = 4, 128, 64, 512
RW_HEADS, RW_HEAD, RW_WIDTH, RW_CHUNK, RW_COLS = 8, 64, 512, 64, 1792
RW_GN_EPS = 64e-5
RW_SUB = 16
RW_GROUP = 4
assert RW_CHUNK == RW_HEAD
PK_HEADS, PK_HALF, N_KEYS, PK_TOPK = 8, 64, 128, 16
HG_COLS, GATE_COLS = 2048, 2048

NT = (((1,), (1,)), ((), ()))
TN = (((0,), (0,)), ((), ()))


def _dot(a, b, dims=None, precision=None):
    if dims is None:
        return jnp.dot(a, b, preferred_element_type=F32, precision=precision)
    return lax.dot_general(a, b, dims, preferred_element_type=F32, precision=precision)


def _bdot(a, b, dims=None):
    return _dot(a.astype(BF16), b.astype(BF16), dims)


def _rms(x, w):
    return x * lax.rsqrt(jnp.mean(x * x, axis=-1, keepdims=True) + RMS_EPS) * w


def _sigmoid(x):
    return 1.0 / (1.0 + jnp.exp(-x))


def _tri(n, strict):
    r = lax.broadcasted_iota(jnp.int32, (n, n), 0)
    c = lax.broadcasted_iota(jnp.int32, (n, n), 1)
    return (c < r) if strict else (c <= r)


def _cumsum_rows(x):
    n = x.shape[0]
    return _dot(_tri(n, False).astype(F32), x, precision=HIGHEST)


def _inproj_kernel(x_ref, nw_ref, w_ref, hg_ref, rw_ref, gt_ref):
    h = _rms(x_ref[...], nw_ref[...]).astype(BF16)

    def emit(out_ref, col0, width):
        for c in range(0, width, 256):
            out_ref[:, c:c + 256] = _dot(h, w_ref[:, col0 + c:col0 + c + 256])

    emit(hg_ref, 0, HG_COLS)
    emit(rw_ref, HG_COLS, RW_COLS)
    emit(gt_ref, HG_COLS + RW_COLS, GATE_COLS)


def _inproj(x2, nw, w_bf, tm=256):
    T, D = x2.shape
    const = lambda i: (0, 0)
    return pl.pallas_call(
        _inproj_kernel,
        grid=(T // tm,),
        in_specs=[pl.BlockSpec((tm, D), lambda i: (i, 0)),
                  pl.BlockSpec((1, D), const),
                  pl.BlockSpec(w_bf.shape, const)],
        out_specs=[pl.BlockSpec((tm, HG_COLS), lambda i: (i, 0)),
                   pl.BlockSpec((tm, RW_COLS), lambda i: (i, 0)),
                   pl.BlockSpec((tm, GATE_COLS), lambda i: (i, 0))],
        out_shape=[jax.ShapeDtypeStruct((T, HG_COLS), F32),
                   jax.ShapeDtypeStruct((T, RW_COLS), F32),
                   jax.ShapeDtypeStruct((T, GATE_COLS), F32)],
        compiler_params=pltpu.CompilerParams(dimension_semantics=("arbitrary",)),
        name="inproj",
    )(x2, nw, w_bf)


def _hgrn_kernel(q_ref, f_ref, i_ref, go_ref, lb_ref, nw_ref, o_ref, st_ref):
    @pl.when(pl.program_id(1) == 0)
    def _():
        st_ref[...] = jnp.zeros_like(st_ref)

    C = HG_CHUNK
    lb = lb_ref[...]
    q, fl, v, go = q_ref[0], f_ref[0], i_ref[0], go_ref[0]
    qs = q * _sigmoid(q) * (HG_DK ** -0.5)
    f = lb + (1.0 - lb) * _sigmoid(fl)
    k = 1.0 - f
    b = _cumsum_rows(jnp.log(f))
    b_last = b[C - 1:C, :]
    qe = qs * jnp.exp(b)
    ke = k * jnp.exp(-b)
    kd = k * jnp.exp(b_last - b)
    dec = jnp.exp(b_last)
    causal = _tri(C, False)
    outs = []
    for h in range(HG_HEADS):
        sl = slice(h * HG_DK, (h + 1) * HG_DK)
        st = st_ref[h]
        qh = qe[:, sl].astype(BF16)
        vh = v[:, sl].astype(BF16)
        att = jnp.where(causal, _bdot(qh, ke[:, sl], NT), 0.0)
        o = _bdot(qh, st, NT) + _bdot(att, vh)
        st_ref[h] = st * dec[:, sl] + _bdot(vh, kd[:, sl], TN)
        outs.append(o * lax.rsqrt(jnp.mean(o * o, axis=-1, keepdims=True) + RMS_EPS))
    o = jnp.concatenate(outs, axis=1) * nw_ref[...] * (go * _sigmoid(go))
    o_ref[0] = o.astype(o_ref.dtype)


def _hgrn(hg3, lb, nw):
    B, S, _ = hg3.shape
    C = HG_CHUNK
    spec = lambda j: pl.BlockSpec((1, C, HG_WIDTH), lambda b, c, j=j: (b, c, j))
    const = lambda b, c: (0, 0)
    return pl.pallas_call(
        _hgrn_kernel,
        grid=(B, S // C),
        in_specs=[spec(0), spec(1), spec(2), spec(3),
                  pl.BlockSpec((1, HG_WIDTH), const), pl.BlockSpec((1, HG_WIDTH), const)],
        out_specs=pl.BlockSpec((1, C, HG_WIDTH), lambda b, c: (b, c, 0)),
        out_shape=jax.ShapeDtypeStruct((B, S, HG_WIDTH), BF16),
        scratch_shapes=[pltpu.VMEM((HG_HEADS, HG_DK, HG_DK), F32)],
        compiler_params=pltpu.CompilerParams(dimension_semantics=("arbitrary", "arbitrary")),
        name="hgrn2",
    )(hg3, hg3, hg3, hg3, lb, nw)


def _softplus(z):
    return jnp.maximum(z, 0.0) + jnp.log(1.0 + jnp.exp(-jnp.abs(z)))


def _head_sums(xs):
    C, n = xs[0].shape
    r = lax.broadcasted_iota(jnp.int32, (n, n), 0) // RW_HEAD
    c = lax.broadcasted_iota(jnp.int32, (n, n), 1) // RW_HEAD
    seg = (r == c).astype(BF16)
    parts = []
    for x in xs:
        hi = x.astype(BF16)
        parts += [hi, (x - hi.astype(F32)).astype(BF16)]
    s = _dot(jnp.concatenate(parts, axis=0), seg)
    return [s[2 * i * C:(2 * i + 1) * C] + s[(2 * i + 1) * C:(2 * i + 2) * C] for i in range(len(xs))]


def _unit_lower_solve(a, x, same_sub, eye, n_sub):
    d = jnp.where(same_sub, a, 0.0)
    t = jnp.where(eye, 1.0, d)
    steps = 2
    while steps < RW_SUB:
        d = _bdot(d, d)
        t = t + _bdot(t, d)
        steps *= 2
    e = _bdot(t, a - jnp.where(same_sub, a, 0.0))
    y = _bdot(t, x)
    steps = 1
    while True:
        y = y + _bdot(e, y)
        steps *= 2
        if steps >= n_sub:
            break
        e = _bdot(e, e)
    return y


def _rwkv_kernel(rw_ref, prev_ref, mu_ref, w0_ref, w2_ref, a0_ref, a2_ref, g2_ref, kk_ref, ka_ref, rk_ref,
                 lnw_ref, lnb_ref, o_ref, st_ref):
    ci = pl.program_id(1)

    @pl.when(ci == 0)
    def _():
        st_ref[...] = jnp.zeros_like(st_ref)

    C, N, W = RW_CHUNK, RW_HEAD, RW_WIDTH
    x = rw_ref[0]
    last_prev = jnp.where(ci == 0, 0.0, prev_ref[0][7:8, :])
    rows = lax.broadcasted_iota(jnp.int32, x.shape, 0)
    prev = jnp.where(rows == 0, last_prev, pltpu.roll(x, 1, axis=0))
    xs = x + (prev - x) * mu_ref[...]
    r, k, v = xs[:, 0:W], xs[:, W:2 * W], xs[:, 2 * W:3 * W]
    wd, ad, gd = xs[:, 3 * W:3 * W + 64], xs[:, 3 * W + 64:3 * W + 128], xs[:, 3 * W + 128:3 * W + 256]
    w_log = -_softplus(-(w0_ref[...] + _bdot(jnp.tanh(wd), w2_ref[...]))) - 0.5
    lw = -jnp.exp(w_log)
    a = _sigmoid(a0_ref[...] + _bdot(ad, a2_ref[...]))
    g = _bdot(_sigmoid(gd), g2_ref[...])
    kk = k * kk_ref[...]
    k = k * (1.0 + (a - 1.0) * ka_ref[...])
    kk_sq, rk_sum = _head_sums([kk * kk, r * k * rk_ref[...]])
    kk = kk / jnp.maximum(jnp.sqrt(kk_sq), 1e-12)
    bonus = rk_sum * v

    al = -kk
    be = kk * a
    cum = _cumsum_rows(lw)
    cum_last = cum[C - 1:C, :]
    e_pos = jnp.exp(cum)
    e_neg = jnp.exp(-cum)
    e_end = jnp.exp(cum_last - cum)
    al_t = al * jnp.exp(cum - lw)
    r_t = r * e_pos
    be_t = be * e_neg
    k_t = k * e_neg
    be_e = be * e_end
    k_e = k * e_end
    gam = jnp.exp(cum_last)

    GW = RW_GROUP * N
    ri = lax.broadcasted_iota(jnp.int32, (GW, GW), 0)
    cj = lax.broadcasted_iota(jnp.int32, (GW, GW), 1)
    same_head = (ri // N) == (cj // N)
    strict = same_head & (cj < ri)
    incl = same_head & (cj <= ri)
    same_sub = (ri // RW_SUB) == (cj // RW_SUB)
    eye = ri == cj

    outs = []
    for gi in range(RW_HEADS // RW_GROUP):
        gs = slice(gi * GW, (gi + 1) * GW)

        def blockdiag(m):
            return jnp.where(same_head, jnp.concatenate([m[:, gs]] * RW_GROUP, axis=0), 0.0).astype(BF16)

        al_b, r_b, be_b, k_b = blockdiag(al_t), blockdiag(r_t), blockdiag(be_t), blockdiag(k_t)
        v_b = blockdiag(v)
        p = _dot(jnp.concatenate([al_b, r_b], axis=0), jnp.concatenate([be_b, k_b], axis=0), NT)
        a_m = jnp.where(strict, p[:GW, :GW], 0.0)
        b_m = jnp.where(strict, p[:GW, GW:], 0.0)
        aq = jnp.where(incl, p[GW:, :GW], 0.0)
        bq = jnp.where(incl, p[GW:, GW:], 0.0)
        z = _unit_lower_solve(a_m, jnp.concatenate([al_b.astype(F32), _bdot(b_m, v_b)], axis=1),
                              same_sub, eye, C // RW_SUB)
        aqz = _bdot(aq, z)
        rq = r_b.astype(F32) + aqz[:, :GW]
        y_loc = aqz[:, GW:] + _bdot(bq, v_b)
        btz = _bdot(blockdiag(be_e), z, TN)
        phi = jnp.where(eye, gam[:, gs], 0.0) + btz[:, :GW]
        psi = btz[:, GW:] + _dot(blockdiag(k_e), v_b, TN)
        st = st_ref[gi]
        y_b = _bdot(rq, st) + y_loc
        st_ref[gi] = _bdot(phi, st) + psi
        outs.append(sum(y_b[h * C:(h + 1) * C] for h in range(RW_GROUP)))
    y = jnp.concatenate(outs, axis=1)
    mean = _head_sums([y])[0] * (1.0 / N)
    yc = y - mean
    var = _head_sums([yc * yc])[0] * (1.0 / N)
    y = yc * lax.rsqrt(var + RW_GN_EPS) * lnw_ref[...] + lnb_ref[...]
    o_ref[0] = ((y + bonus) * g).astype(o_ref.dtype)


def _rwkv(rw3, mu, w0, w2, a0, a2, g2, k_k, k_a, r_k, ln_w, ln_b):
    B, S, _ = rw3.shape
    C = RW_CHUNK
    const = lambda b, c: (0, 0)
    full = lambda arr: pl.BlockSpec(arr.shape, const)
    return pl.pallas_call(
        _rwkv_kernel,
        grid=(B, S // C),
        in_specs=[pl.BlockSpec((1, C, RW_COLS), lambda b, c: (b, c, 0)),
                  pl.BlockSpec((1, 8, RW_COLS), lambda b, c: (b, jnp.maximum(c * (C // 8) - 1, 0), 0)),
                  full(mu), full(w0), full(w2), full(a0), full(a2), full(g2), full(k_k), full(k_a), full(r_k),
                  full(ln_w), full(ln_b)],
        out_specs=pl.BlockSpec((1, C, RW_WIDTH), lambda b, c: (b, c, 0)),
        out_shape=jax.ShapeDtypeStruct((B, S, RW_WIDTH), BF16),
        scratch_shapes=[pltpu.VMEM((RW_HEADS // RW_GROUP, RW_GROUP * RW_HEAD, RW_GROUP * RW_HEAD), F32)],
        compiler_params=pltpu.CompilerParams(dimension_semantics=("arbitrary", "arbitrary")),
        name="rwkv7",
    )(rw3, rw3, mu, w0, w2, a0, a2, g2, k_k, k_a, r_k, ln_w, ln_b)


def _merge_kernel(x_ref, yhg_ref, yrw_ref, gt_ref, wbh_ref, wbr_ref, wout_ref, nw_ref, wqt_ref,
                  x1_ref, h2_ref, qt_ref):
    D = x_ref.shape[1]
    merged = (_sigmoid(gt_ref[:, :D]) * _dot(yhg_ref[...], wbh_ref[...])
              + _sigmoid(gt_ref[:, D:]) * _dot(yrw_ref[...], wbr_ref[...]))
    x1 = x_ref[...] + _dot(merged.astype(BF16), wout_ref[...])
    x1_ref[...] = x1
    h2 = _rms(x1, nw_ref[...]).astype(BF16)
    h2_ref[...] = h2
    qt_ref[...] = _dot(wqt_ref[...], h2, NT)


def _merge(x2, yhg, yrw, gt, wbh, wbr, wout, nw, wqt, tm=256):
    T, D = x2.shape
    const = lambda i: (0, 0)
    row = lambda w: pl.BlockSpec((tm, w), lambda i: (i, 0))
    full = lambda arr: pl.BlockSpec(arr.shape, const)
    return pl.pallas_call(
        _merge_kernel,
        grid=(T // tm,),
        in_specs=[row(D), row(HG_WIDTH), row(RW_WIDTH), row(GATE_COLS),
                  full(wbh), full(wbr), full(wout), full(nw), full(wqt)],
        out_specs=[row(D), row(D), pl.BlockSpec((wqt.shape[0], tm), lambda i: (0, i))],
        out_shape=[jax.ShapeDtypeStruct((T, D), F32), jax.ShapeDtypeStruct((T, D), BF16),
                   jax.ShapeDtypeStruct((wqt.shape[0], T), F32)],
        compiler_params=pltpu.CompilerParams(dimension_semantics=("arbitrary",)),
        name="merge",
    )(x2, yhg, yrw, gt, wbh, wbr, wout, nw, wqt)


def _top16_rows(s, tie, payload=None):
    big = jnp.iinfo(jnp.int32).max
    vals, picks = [], []
    for _ in range(PK_TOPK):
        m = jnp.max(s, axis=0, keepdims=True)
        idx = jnp.min(jnp.where(s == m, tie, big), axis=0, keepdims=True)
        sel = tie == idx
        vals.append(m)
        picks.append(idx if payload is None else jnp.max(jnp.where(sel, payload, -1.0), axis=0, keepdims=True))
        s = jnp.where(sel, -jnp.inf, s)
    return jnp.concatenate(vals, axis=0), jnp.concatenate(picks, axis=0)


def _pair_candidates():
    width = [16, 8, 8, 4, 4, 2, 2, 2] + [1] * 8
    pairs = [(i, j) for i in range(PK_TOPK) for j in range(width[i])]
    assert all((i, j) in pairs for i in range(PK_TOPK) for j in range(PK_TOPK) if (i + 1) * (j + 1) <= PK_TOPK)
    return pairs + [None] * (-len(pairs) % 8)


def _select_kernel(qt_ref, keys_ref, sa_ref, sb_ref, tie_ref, i1_ref, i2_ref, g_ref):
    qt = qt_ref[...]
    s1 = _dot(keys_ref[0], qt[:PK_HALF, :], precision=HIGHEST)
    s2 = _dot(keys_ref[1], qt[PK_HALF:, :], precision=HIGHEST)
    rows = lax.broadcasted_iota(jnp.int32, s1.shape, 0)
    a, ia = _top16_rows(s1, rows)
    b, ib = _top16_rows(s2, rows)
    expand = lambda m, x: _dot(m, x, precision=HIGHEST)
    tie = tie_ref[...]
    cand = jnp.where(tie < PK_TOPK * PK_TOPK, expand(sa_ref[...], a) + expand(sb_ref[...], b), -jnp.inf)
    eid = expand(sa_ref[...], ia.astype(F32)) * N_KEYS + expand(sb_ref[...], ib.astype(F32))
    cs, ce = _top16_rows(cand, tie, eid)
    ex = jnp.exp(cs - cs[0:1, :])
    g_ref[...] = ex / jnp.sum(ex, axis=0, keepdims=True)
    i1 = jnp.floor(ce * (1.0 / N_KEYS))
    i1_ref[...] = i1
    i2_ref[...] = ce - i1 * N_KEYS


def _select(qt, keys, tl=256):
    R, T = qt.shape
    heads = R // (2 * PK_HALF)
    pairs = _pair_candidates()
    onehot = lambda which: jnp.array([[float(p is not None and p[which] == i) for i in range(PK_TOPK)] for p in pairs], F32)
    tie = jnp.array([[PK_TOPK * PK_TOPK if p is None else p[0] * PK_TOPK + p[1]] * tl for p in pairs], jnp.int32)
    out = jax.ShapeDtypeStruct((heads * PK_TOPK, T), F32)
    ospec = pl.BlockSpec((PK_TOPK, tl), lambda i, h: (h, i))
    const = lambda i, h: (0, 0)
    return pl.pallas_call(
        _select_kernel,
        grid=(T // tl, heads),
        in_specs=[pl.BlockSpec((2 * PK_HALF, tl), lambda i, h: (h, i)),
                  pl.BlockSpec(keys.shape, lambda i, h: (0, 0, 0)),
                  pl.BlockSpec((len(pairs), PK_TOPK), const), pl.BlockSpec((len(pairs), PK_TOPK), const),
                  pl.BlockSpec(tie.shape, const)],
        out_specs=[ospec, ospec, ospec],
        out_shape=[out, out, out],
        compiler_params=pltpu.CompilerParams(dimension_semantics=("arbitrary", "arbitrary")),
        name="peer_select",
    )(qt, keys, onehot(0), onehot(1), tie)


def _gbuild_kernel(i1_ref, i2_ref, g_ref, o_hbm, i1s, i2s, gs, buf, sem):
    i1s[...] = i1_ref[...].T
    i2s[...] = i2_ref[...].T
    gs[...] = g_ref[...].T
    step, n_steps = pl.program_id(0), pl.num_programs(0)
    slot = step % 2
    tg = buf.shape[1]
    sub = lax.broadcasted_iota(jnp.int32, (N_KEYS, i1s.shape[1]), 0).astype(F32)

    def relayout(s, t, tok):
        return pltpu.make_async_copy(buf.at[s, t], o_hbm.at[:, tok, :], sem.at[s])

    def drain(s):
        def wait_one(t, carry):
            relayout(s, t, 0).wait()
            return carry
        lax.fori_loop(0, tg, wait_one, 0)

    @pl.when(step >= 2)
    def _():
        drain(slot)

    def body(t, carry):
        r1 = i1s[pl.ds(t, 1), :]
        r2 = i2s[pl.ds(t, 1), :]
        gg = gs[pl.ds(t, 1), :]
        at = jnp.where(sub == r1, gg, 0.0).astype(BF16)
        bt = jnp.where(sub == r2, 1.0, 0.0).astype(BF16)
        buf[slot, t] = _dot(at, bt, NT)
        relayout(slot, t, step * tg + t).start()
        return carry

    lax.fori_loop(0, tg, body, 0, unroll=8)

    @pl.when(step == n_steps - 1)
    def _():
        drain(slot)

    @pl.when((step == n_steps - 1) & (step >= 1))
    def _():
        drain(1 - slot)


def _gbuild(i1, i2, g, tg=128):
    J, T = i1.shape
    ispec = pl.BlockSpec((J, tg), lambda i: (0, i))
    return pl.pallas_call(
        _gbuild_kernel,
        grid=(T // tg,),
        in_specs=[ispec, ispec, ispec],
        out_specs=pl.BlockSpec(memory_space=pl.ANY),
        out_shape=jax.ShapeDtypeStruct((N_KEYS, T, N_KEYS), F32),
        scratch_shapes=[pltpu.VMEM((tg, J), F32)] * 3 + [pltpu.VMEM((2, tg, N_KEYS, N_KEYS), F32),
                                                        pltpu.SemaphoreType.DMA((2,))],
        compiler_params=pltpu.CompilerParams(dimension_semantics=("arbitrary",)),
        name="peer_gates",
    )(i1, i2, g)


def _gelu(x):
    return 0.5 * x * (1.0 + lax.erf(x * (2.0 ** -0.5)))


def _peer_kernel(h_ref, g_ref, u_ref, v_ref, x1_ref, o_ref, acc_ref, w_ref):
    j = pl.program_id(1)

    @pl.when(j == 0)
    def _():
        acc_ref[...] = jnp.zeros_like(acc_ref)

    act = _dot(h_ref[...], u_ref[...], NT)
    for c in range(g_ref.shape[0]):
        sl = slice(c * N_KEYS, (c + 1) * N_KEYS)
        w_ref[:, sl] = (g_ref[c] * _gelu(act[:, sl])).astype(BF16)
    acc_ref[...] += _dot(w_ref[...], v_ref[...])

    @pl.when(j == pl.num_programs(1) - 1)
    def _():
        o_ref[...] = x1_ref[...] + acc_ref[...]


def _peer(h2, g3, u_bf, v_bf, x1, tm=512, te=1024):
    T, D = h2.shape
    E = u_bf.shape[0]
    return pl.pallas_call(
        _peer_kernel,
        grid=(T // tm, E // te),
        in_specs=[pl.BlockSpec((tm, D), lambda i, j: (i, 0)),
                  pl.BlockSpec((te // N_KEYS, tm, N_KEYS), lambda i, j: (j, i, 0)),
                  pl.BlockSpec((te, D), lambda i, j: (j, 0)),
                  pl.BlockSpec((te, D), lambda i, j: (j, 0)),
                  pl.BlockSpec((tm, D), lambda i, j: (i, 0))],
        out_specs=pl.BlockSpec((tm, D), lambda i, j: (i, 0)),
        out_shape=jax.ShapeDtypeStruct((T, D), F32),
        scratch_shapes=[pltpu.VMEM((tm, D), F32), pltpu.VMEM((tm, te), BF16)],
        compiler_params=pltpu.CompilerParams(dimension_semantics=("arbitrary", "arbitrary")),
        name="peer_dense",
    )(h2, g3, u_bf, v_bf, x1)


def _ple_kernel(x_ref, p_ref, nw_ref, wg_ref, wp_ref, fw_ref, o_ref, *, final):
    x = x_ref[...]
    gate = _sigmoid(_dot(_rms(x, nw_ref[...]).astype(BF16), wg_ref[...]))
    x = x + gate * _dot(p_ref[...].astype(BF16), wp_ref[...])
    o_ref[...] = _rms(x, fw_ref[...]) if final else x


def _ple(x2, p2, nw, wg, wp, fw, final, tm=512):
    T, D = x2.shape
    const = lambda i: (0, 0)
    full = lambda arr: pl.BlockSpec(arr.shape, const)
    return pl.pallas_call(
        functools.partial(_ple_kernel, final=final),
        grid=(T // tm,),
        in_specs=[pl.BlockSpec((tm, D), lambda i: (i, 0)), pl.BlockSpec((tm, p2.shape[1]), lambda i: (i, 0)),
                  full(nw), full(wg), full(wp), full(fw)],
        out_specs=pl.BlockSpec((tm, D), lambda i: (i, 0)),
        out_shape=jax.ShapeDtypeStruct((T, D), F32),
        compiler_params=pltpu.CompilerParams(dimension_semantics=("arbitrary",)),
        name="ple_final",
    )(x2, p2, nw, wg, wp, fw)


def kernel(x, p, norm_mix_w, w_in, hg_lb, hg_norm_w, rw_mu, rw_w0, rw_w2, rw_a0, rw_a2, rw_g2, rw_k_k, rw_k_a, rw_r_k, rw_ln_w, rw_ln_b, w_branch_hg, w_branch_rw, w_out, norm_ffn_w, peer_wq, peer_keys, peer_u, peer_v, norm_ple_w, ple_proj, ple_gate, final_norm_w):
    B, S, D = x.shape
    T = B * S
    depth = w_in.shape[0]
    row = lambda a: a.reshape(1, -1).astype(F32)
    lower_bounds = jnp.cumsum(jax.nn.softmax(hg_lb.astype(F32), axis=0), axis=0)
    xf = x.reshape(T, D)
    for i in range(depth):
        hg, rw, gt = _inproj(xf, row(norm_mix_w[i]), w_in[i].astype(BF16))
        y_hg = _hgrn(hg.reshape(B, S, HG_COLS), row(lower_bounds[i]), row(hg_norm_w[i]))
        y_rw = _rwkv(rw.reshape(B, S, RW_COLS), row(rw_mu[i]), row(rw_w0[i]), rw_w2[i].astype(BF16),
                     row(rw_a0[i]), rw_a2[i].astype(BF16), rw_g2[i].astype(BF16), row(rw_k_k[i]),
                     row(rw_k_a[i]), row(rw_r_k[i]), row(rw_ln_w[i]), row(rw_ln_b[i]))
        x1, h2, qt = _merge(xf, y_hg.reshape(T, HG_WIDTH), y_rw.reshape(T, RW_WIDTH), gt,
                            w_branch_hg[i].astype(BF16), w_branch_rw[i].astype(BF16), w_out[i].astype(BF16),
                            row(norm_ffn_w[i]), peer_wq[i].T.astype(BF16))
        i1, i2, g = _select(qt, peer_keys[i].astype(F32))
        g3 = _gbuild(i1, i2, g)
        x2 = _peer(h2, g3, peer_u[i].astype(BF16), peer_v[i].astype(BF16), x1)
        xf = _ple(x2, p[i].reshape(T, -1), row(norm_ple_w[i]), ple_gate[i].astype(BF16),
                  ple_proj[i].astype(BF16), row(final_norm_w), final=(i == depth - 1))
    return xf.reshape(B, S, D)
```
